```python
import math
import jax
import jax.numpy as jnp
from jax import lax
import numpy as np

D_MODEL = 1024
BATCH = 32
SEQ = 2048
DEPTH = 4
DEC_BATCH = 2
DEC_SEQ = 8192
PAST_LEN = 128

N_HEADS = 16
QK_NOPE = 128
QK_ROPE = 64
QK_HEAD = QK_NOPE + QK_ROPE
V_HEAD = 128
Q_LORA = 256
KV_LORA = 128
ROPE_THETA = 10000.0
Q_BLOCK = 128
ATTN_SCALE = 1.0 / math.sqrt(QK_HEAD)
D_CONV = D_MODEL
CONV_WIDTH = 31
CONV_PAD = (CONV_WIDTH - 1) // 2
D_FF = 2816
N_EXPERTS = 8
TOP_K = 2
D_EXPERT = 3584
RMS_EPS = 1e-6
N_A = (DEPTH + 1) // 2
N_B = DEPTH // 2

kernel_name = "hybrid_mla_conformer_moe_encoder"


def _rms_norm(x, g):
    xf = x.astype(jnp.float32)
    y = xf * lax.rsqrt(jnp.mean(xf * xf, axis=-1, keepdims=True) + RMS_EPS)
    return (y * g.astype(jnp.float32)).astype(x.dtype)


def _rope_tables(seq):
    inv_freq = ROPE_THETA ** (-jnp.arange(0, QK_ROPE, 2, dtype=jnp.float32) / QK_ROPE)
    pos = jnp.arange(seq, dtype=jnp.float32)
    ang = pos[:, None] * inv_freq[None, :]
    ang = jnp.concatenate([ang, ang], axis=-1)[:, None, :]
    return jnp.cos(ang), jnp.sin(ang)


def _apply_rope(x, cos, sin):
    x1, x2 = jnp.split(x, 2, axis=-1)
    rot = jnp.concatenate([-x2, x1], axis=-1)
    return (x.astype(jnp.float32) * cos + rot.astype(jnp.float32) * sin).astype(x.dtype)


def _block_attention(q, k, v):
    B, S, H, _ = q.shape
    nb = S // Q_BLOCK
    qb = q.reshape(B, nb, Q_BLOCK, H, QK_HEAD).transpose(1, 0, 2, 3, 4)

    def one_block(qi):
        s = jnp.einsum('bqhd,bkhd->bhqk', qi, k, preferred_element_type=jnp.float32) * ATTN_SCALE
        p = jax.nn.softmax(s, axis=-1)
        return jnp.einsum('bhqk,bkhv->bqhv', p.astype(v.dtype), v)

    o = lax.map(one_block, qb)
    return o.transpose(1, 0, 2, 3, 4).reshape(B, S, H * V_HEAD)


def _mla(x, w_in, q_lat_g, kv_lat_g, w_uq, w_ukv, q_norm_g, k_norm_g, w_o):
    B, S, _ = x.shape
    a = x @ w_in
    c_q, c_kv, k_pe = jnp.split(a, [Q_LORA, Q_LORA + KV_LORA], axis=-1)
    c_q = _rms_norm(c_q, q_lat_g)
    c_kv = _rms_norm(c_kv, kv_lat_g)
    q = (c_q @ w_uq).reshape(B, S, N_HEADS, QK_HEAD)
    kv = (c_kv @ w_ukv).reshape(B, S, N_HEADS, QK_NOPE + V_HEAD)
    q_nope, q_pe = jnp.split(q, [QK_NOPE], axis=-1)
    k_nope, v = jnp.split(kv, [QK_NOPE], axis=-1)
    q_nope = _rms_norm(q_nope, q_norm_g[:QK_NOPE])
    q_pe = _rms_norm(q_pe, q_norm_g[QK_NOPE:])
    k_nope = _rms_norm(k_nope, k_norm_g[:QK_NOPE])
    k_pe = _rms_norm(k_pe[:, :, None, :], k_norm_g[QK_NOPE:])
    cos, sin = _rope_tables(S)
    q = jnp.concatenate([q_nope, _apply_rope(q_pe, cos, sin)], axis=-1)
    k_pe = jnp.broadcast_to(_apply_rope(k_pe, cos, sin), (B, S, N_HEADS, QK_ROPE))
    k = jnp.concatenate([k_nope, k_pe], axis=-1)
    o = _block_attention(q, k, v)
    return o @ w_o


def _conformer_conv(x, w_pw1, b_pw1, w_dw, b_dw, norm_g, w_pw2):
    h = x @ w_pw1 + b_pw1
    a, b = jnp.split(h, 2, axis=-1)
    h = a * jax.nn.sigmoid(b)
    h = lax.conv_general_dilated(
        h, w_dw[:, None, :].astype(h.dtype), window_strides=(1,),
        padding=[(CONV_PAD, CONV_PAD)], dimension_numbers=('NWC', 'WIO', 'NWC'),
        feature_group_count=D_CONV) + b_dw
    h = jax.nn.silu(_rms_norm(h, norm_g))
    return h @ w_pw2


def _swiglu(x, w_in, w_out):
    g, u = jnp.split(x @ w_in, 2, axis=-1)
    return (jax.nn.silu(g) * u) @ w_out


def _moe(x, w_router, w_in, w_out):
    B, S, D = x.shape
    t = x.reshape(B * S, D)
    logits = (t @ w_router).astype(jnp.float32)
    top_v, top_i = lax.top_k(logits, TOP_K)
    wts = jax.nn.softmax(top_v, axis=-1)
    gates = jnp.sum(jax.nn.one_hot(top_i, N_EXPERTS, dtype=jnp.float32) * wts[..., None], axis=1)
    gates = gates.astype(x.dtype)
    y = jnp.zeros_like(t)
    for e in range(N_EXPERTS):
        y = y + gates[:, e:e + 1] * _swiglu(t, w_in[e], w_out[e])
    return y.reshape(B, S, D)


def _trunk(x, norm_mix_g, norm_ffn_g,
           mla_w_in, mla_q_lat_g, mla_kv_lat_g, mla_w_uq, mla_w_ukv, mla_q_norm_g, mla_k_norm_g, mla_w_o,
           conv_w_pw1, conv_b_pw1, conv_w_dw, conv_b_dw, conv_norm_g, conv_w_pw2,
           ffn_w_in, ffn_w_out, moe_w_router, moe_w_in, moe_w_out):
    for i in range(DEPTH):
        j = i // 2
        h = _rms_norm(x, norm_mix_g[i])
        if i % 2 == 0:
            x = x + _mla(h, mla_w_in[j], mla_q_lat_g[j], mla_kv_lat_g[j], mla_w_uq[j], mla_w_ukv[j],
                         mla_q_norm_g[j], mla_k_norm_g[j], mla_w_o[j])
        else:
            x = x + _conformer_conv(h, conv_w_pw1[j], conv_b_pw1[j], conv_w_dw[j], conv_b_dw[j],
                                    conv_norm_g[j], conv_w_pw2[j])
        h = _rms_norm(x, norm_ffn_g[i])
        if i % 2 == 0:
            x = x + _swiglu(h, ffn_w_in[j], ffn_w_out[j])
        else:
            x = x + _moe(h, moe_w_router[j], moe_w_in[j], moe_w_out[j])
    return x


def setup_inputs(seed: int = 0) -> dict:
    key = jax.random.key(seed)
    ks = jax.random.split(key, 24)

    def nrm(k, shape, scale):
        return jax.random.normal(k, shape, jnp.float32) * scale

    def gain(k, shape):
        return 1.0 + 0.05 * jax.random.normal(k, shape, jnp.float32)

    return {
        "x_prompt": nrm(ks[0], (BATCH, SEQ, D_MODEL), 1.0),
        "x_sample": nrm(ks[1], (DEC_BATCH, DEC_SEQ, D_MODEL), 1.0),
        "norm_mix_g": gain(ks[2], (DEPTH, D_MODEL)),
        "norm_ffn_g": gain(ks[3], (DEPTH, D_MODEL)),
        "mla_w_in": nrm(ks[4], (N_A, D_MODEL, Q_LORA + KV_LORA + QK_ROPE), D_MODEL ** -0.5),
        "mla_q_lat_g": gain(ks[5], (N_A, Q_LORA)),
        "mla_kv_lat_g": gain(ks[6], (N_A, KV_LORA)),
        "mla_w_uq": nrm(ks[7], (N_A, Q_LORA, N_HEADS * QK_HEAD), Q_LORA ** -0.5),
        "mla_w_ukv": nrm(ks[8], (N_A, KV_LORA, N_HEADS * (QK_NOPE + V_HEAD)), KV_LORA ** -0.5),
        "mla_q_norm_g": gain(ks[9], (N_A, QK_HEAD)),
        "mla_k_norm_g": gain(ks[10], (N_A, QK_HEAD)),
        "mla_w_o": nrm(ks[11], (N_A, N_HEADS * V_HEAD, D_MODEL), (N_HEADS * V_HEAD) ** -0.5),
        "conv_w_pw1": nrm(ks[12], (N_B, D_MODEL, 2 * D_CONV), D_MODEL ** -0.5),
        "conv_b_pw1": nrm(ks[13], (N_B, 2 * D_CONV), 0.02),
        "conv_w_dw": nrm(ks[14], (N_B, CONV_WIDTH, D_CONV), CONV_WIDTH ** -0.5),
        "conv_b_dw": nrm(ks[15], (N_B, D_CONV), 0.02),
        "conv_norm_g": gain(ks[16], (N_B, D_CONV)),
        "conv_w_pw2": nrm(ks[17], (N_B, D_CONV, D_MODEL), D_CONV ** -0.5),
        "ffn_w_in": nrm(ks[18], (N_A, D_MODEL, 2 * D_FF), D_MODEL ** -0.5),
        "ffn_w_out": nrm(ks[19], (N_A, D_FF, D_MODEL), D_FF ** -0.5),
        "moe_w_router": nrm(ks[20], (N_B, D_MODEL, N_EXPERTS), D_MODEL ** -0.5),
        "moe_w_in": nrm(ks[21], (N_B, N_EXPERTS, D_MODEL, 2 * D_EXPERT), D_MODEL ** -0.5),
        "moe_w_out": nrm(ks[22], (N_B, N_EXPERTS, D_EXPERT, D_MODEL), D_EXPERT ** -0.5),
    }


def reference(x_prompt, x_sample, norm_mix_g, norm_ffn_g,
              mla_w_in, mla_q_lat_g, mla_kv_lat_g, mla_w_uq, mla_w_ukv, mla_q_norm_g, mla_k_norm_g, mla_w_o,
              conv_w_pw1, conv_b_pw1, conv_w_dw, conv_b_dw, conv_norm_g, conv_w_pw2,
              ffn_w_in, ffn_w_out, moe_w_router, moe_w_in, moe_w_out):
    y_prompt = _trunk(x_prompt, norm_mix_g, norm_ffn_g,
                      mla_w_in, mla_q_lat_g, mla_kv_lat_g, mla_w_uq, mla_w_ukv, mla_q_norm_g, mla_k_norm_g, mla_w_o,
                      conv_w_pw1, conv_b_pw1, conv_w_dw, conv_b_dw, conv_norm_g, conv_w_pw2,
                      ffn_w_in, ffn_w_out, moe_w_router, moe_w_in, moe_w_out)
    y_sample = _trunk(x_sample, norm_mix_g, norm_ffn_g,
                      mla_w_in, mla_q_lat_g, mla_kv_lat_g, mla_w_uq, mla_w_ukv, mla_q_norm_g, mla_k_norm_g, mla_w_o,
                      conv_w_pw1, conv_b_pw1, conv_w_dw, conv_b_dw, conv_norm_g, conv_w_pw2,
                      ffn_w_in, ffn_w_out, moe_w_router, moe_w_in, moe_w_out)
    return (y_prompt, y_sample)
```

```python
import functools
import math

import jax
import jax.numpy as jnp
from jax import lax
from jax.experimental import pallas as pl
from jax.experimental.pallas import tpu as pltpu

N_HEADS = 16
QK_NOPE = 128
QK_ROPE = 64
V_HEAD = 128
Q_LORA = 256
KV_LORA = 128
ROPE_THETA = 10000.0
ATTN_SCALE = 1.0 / math.sqrt(QK_NOPE + QK_ROPE)
CONV_WIDTH = 31
CONV_PAD = (CONV_WIDTH - 1) // 2
N_EXPERTS = 8
RMS_EPS = 1e-6

LANE = 128
HALO = 16
VMEM_LIMIT = 56 * 1024 * 1024

F32 = jnp.float32
BF16 = jnp.bfloat16


def _params(*sem):
    return pltpu.CompilerParams(dimension_semantics=sem, vmem_limit_bytes=VMEM_LIMIT)


def _rms(xf, g):
    return xf * lax.rsqrt(jnp.mean(xf * xf, axis=-1, keepdims=True) + RMS_EPS) * g


def _rope_norm(p, r, g, gp, cos, sin):
    rs = lax.rsqrt(jnp.sum(p * p, axis=-1, keepdims=True) * (1.0 / QK_ROPE) + RMS_EPS)
    return (p * g * cos + r * gp * sin) * rs


def _dot(a, b):
    return jnp.dot(a, b, preferred_element_type=F32)


def _mla_in_kernel(x_ref, g_ref, w_ref, gq_ref, gkv_ref, gk_ref, gkp_ref, cos_ref, sin_ref,
                   cq_ref, ckv_ref, kpe_ref):
    h = _rms(x_ref[0], g_ref[...]).astype(BF16)
    a = _dot(h, w_ref[...])
    cq_ref[0] = _rms(a[:, :Q_LORA], gq_ref[...]).astype(BF16)
    ckv_ref[0] = _rms(a[:, Q_LORA:Q_LORA + KV_LORA], gkv_ref[...]).astype(BF16)
    o = Q_LORA + KV_LORA
    kpe = _rope_norm(a[:, o:o + LANE], a[:, o + LANE:o + 2 * LANE],
                     gk_ref[...], gkp_ref[...], cos_ref[...], sin_ref[...])
    kpe_ref[0] = kpe.astype(BF16)


def _mla_in(x, g, w, gq, gkv, gk, gkp, cos, sin, ts):
    B, S, D = x.shape
    n = w.shape[1]
    vec = lambda c: pl.BlockSpec((1, c), lambda b, i: (0, 0))
    return pl.pallas_call(
        _mla_in_kernel,
        grid=(B, S // ts),
        in_specs=[
            pl.BlockSpec((1, ts, D), lambda b, i: (b, i, 0)),
            vec(D),
            pl.BlockSpec((D, n), lambda b, i: (0, 0)),
            vec(Q_LORA), vec(KV_LORA), vec(LANE), vec(LANE),
            pl.BlockSpec((ts, LANE), lambda b, i: (i, 0)),
            pl.BlockSpec((ts, LANE), lambda b, i: (i, 0)),
        ],
        out_specs=[
            pl.BlockSpec((1, ts, Q_LORA), lambda b, i: (b, i, 0)),
            pl.BlockSpec((1, ts, KV_LORA), lambda b, i: (b, i, 0)),
            pl.BlockSpec((1, ts, LANE), lambda b, i: (b, i, 0)),
        ],
        out_shape=[
            jax.ShapeDtypeStruct((B, S, Q_LORA), BF16),
            jax.ShapeDtypeStruct((B, S, KV_LORA), BF16),
            jax.ShapeDtypeStruct((B, S, LANE), BF16),
        ],
        compiler_params=_params("parallel", "parallel"),
        name="mla_in",
    )(x, g, w, gq, gkv, gk, gkp, cos, sin)


def _attn_kernel(cq_ref, ckv_ref, kpe_ref, wq_ref, wkv_ref, gqn_ref, gqp_ref, gqpp_ref, gkn_ref,
                 cos_ref, sin_ref, o_ref, k_scr, v_scr, *, seq, tq, qc, kc):
    @pl.when(pl.program_id(2) == 0)
    def _build_kv():
        def body(i, carry):
            r0 = pl.multiple_of(i * kc, kc)
            kv = _dot(ckv_ref[0, pl.ds(r0, kc), :], wkv_ref[0])
            k_scr[pl.ds(r0, kc), 0:QK_NOPE] = _rms(kv[:, :QK_NOPE], gkn_ref[...]).astype(BF16)
            k_scr[pl.ds(r0, kc), QK_NOPE:QK_NOPE + LANE] = kpe_ref[0, pl.ds(r0, kc), :]
            v_scr[pl.ds(r0, kc), :] = kv[:, QK_NOPE:].astype(BF16)
            return carry
        lax.fori_loop(0, seq // kc, body, 0)

    def q_body(j, carry):
        r0 = pl.multiple_of(j * qc, qc)
        q3 = _dot(cq_ref[0, pl.ds(r0, qc), :], wq_ref[0])
        qn = _rms(q3[:, :QK_NOPE], gqn_ref[...])
        qp = _rope_norm(q3[:, QK_NOPE:QK_NOPE + LANE], q3[:, QK_NOPE + LANE:],
                        gqp_ref[...], gqpp_ref[...],
                        cos_ref[pl.ds(r0, qc), :], sin_ref[pl.ds(r0, qc), :])
        q = (jnp.concatenate([qn, qp], axis=-1) * ATTN_SCALE).astype(BF16)
        m = jnp.full((qc, 1), -jnp.inf, F32)
        l = jnp.zeros((qc, 1), F32)
        acc = jnp.zeros((qc, V_HEAD), F32)
        for c in range(seq // kc):
            k = k_scr[c * kc:(c + 1) * kc, :]
            s = lax.dot_general(q, k, (((1,), (1,)), ((), ())), preferred_element_type=F32)
            m_new = jnp.maximum(m, jnp.max(s, axis=-1, keepdims=True))
            alpha = jnp.exp(m - m_new)
            p = jnp.exp(s - m_new)
            l = alpha * l + jnp.sum(p, axis=-1, keepdims=True)
            acc = alpha * acc + _dot(p.astype(BF16), v_scr[c * kc:(c + 1) * kc, :])
            m = m_new
        o_ref[0, pl.ds(r0, qc), :] = (acc / l).astype(BF16)
        return carry
    lax.fori_loop(0, tq // qc, q_body, 0)


def _attention(cq, ckv, kpe, wq, wkv, gqn, gqp, gqpp, gkn, cos, sin, tq, qc, kc):
    B, S, _ = cq.shape
    H = wq.shape[0]
    vec = lambda: pl.BlockSpec((1, LANE), lambda b, h, i: (0, 0))
    kern = functools.partial(_attn_kernel, seq=S, tq=tq, qc=qc, kc=kc)
    return pl.pallas_call(
        kern,
        grid=(B, H, S // tq),
        in_specs=[
            pl.BlockSpec((1, tq, Q_LORA), lambda b, h, i: (b, i, 0)),
            pl.BlockSpec((1, S, KV_LORA), lambda b, h, i: (b, 0, 0)),
            pl.BlockSpec((1, S, LANE), lambda b, h, i: (b, 0, 0)),
            pl.BlockSpec((1, Q_LORA, 3 * LANE), lambda b, h, i: (h, 0, 0)),
            pl.BlockSpec((1, KV_LORA, QK_NOPE + V_HEAD), lambda b, h, i: (h, 0, 0)),
            vec(), vec(), vec(), vec(),
            pl.BlockSpec((tq, LANE), lambda b, h, i: (i, 0)),
            pl.BlockSpec((tq, LANE), lambda b, h, i: (i, 0)),
        ],
        out_specs=pl.BlockSpec((1, tq, V_HEAD), lambda b, h, i: (b, i, h)),
        out_shape=jax.ShapeDtypeStruct((B, S, H * V_HEAD), BF16),
        scratch_shapes=[pltpu.VMEM((S, QK_NOPE + LANE), BF16), pltpu.VMEM((S, V_HEAD), BF16)],
        compiler_params=_params("parallel", "parallel", "arbitrary"),
        name="mla_attention",
    )(cq, ckv, kpe, wq, wkv, gqn, gqp, gqpp, gkn, cos, sin)


def _proj_res_kernel(a_ref, w_ref, x_ref, o_ref):
    o_ref[...] = x_ref[...] + _dot(a_ref[...], w_ref[...])


def _proj_res(a, w, x, tm):
    T, K = a.shape
    D = w.shape[1]
    return pl.pallas_call(
        _proj_res_kernel,
        grid=(T // tm,),
        in_specs=[
            pl.BlockSpec((tm, K), lambda i: (i, 0)),
            pl.BlockSpec((K, D), lambda i: (0, 0)),
            pl.BlockSpec((tm, D), lambda i: (i, 0)),
        ],
        out_specs=pl.BlockSpec((tm, D), lambda i: (i, 0)),
        out_shape=jax.ShapeDtypeStruct((T, D), F32),
        compiler_params=_params("parallel"),
        name="proj_residual",
    )(a, w, x)


def _swiglu_acc(h, wg_ref, wu_ref, wo_ref, tc, scale, out_ref):
    tf = wg_ref.shape[-1]
    for c in range(tf // tc):
        sl = slice(c * tc, (c + 1) * tc)
        g = _dot(h, wg_ref[0, :, sl])
        u = _dot(h, wu_ref[0, :, sl])
        a = (g * jax.nn.sigmoid(g) * u).astype(BF16)
        d = _dot(a, wo_ref[0, sl, :])
        out_ref[...] += d if scale is None else d * scale


def _ffn_kernel(x_ref, g_ref, wg_ref, wu_ref, wo_ref, out_ref, h_scr, *, tc):
    @pl.when(pl.program_id(1) == 0)
    def _init():
        xf = x_ref[...]
        h_scr[...] = _rms(xf, g_ref[...]).astype(BF16)
        out_ref[...] = xf
    _swiglu_acc(h_scr[...], wg_ref, wu_ref, wo_ref, tc, None, out_ref)


def _ffn(x, g, w_in, w_out, tm, tf, tc):
    T, D = x.shape
    F = w_out.shape[1]
    nf = F // tf
    return pl.pallas_call(
        functools.partial(_ffn_kernel, tc=tc),
        grid=(T // tm, nf),
        in_specs=[
            pl.BlockSpec((tm, D), lambda i, j: (i, 0)),
            pl.BlockSpec((1, D), lambda i, j: (0, 0)),
            pl.BlockSpec((1, D, tf), lambda i, j: (0, 0, j)),
            pl.BlockSpec((1, D, tf), lambda i, j: (0, 0, j + nf)),
            pl.BlockSpec((1, tf, D), lambda i, j: (0, j, 0)),
        ],
        out_specs=pl.BlockSpec((tm, D), lambda i, j: (i, 0)),
        out_shape=jax.ShapeDtypeStruct((T, D), F32),
        scratch_shapes=[pltpu.VMEM((tm, D), BF16)],
        compiler_params=_params("parallel", "arbitrary"),
        name="ffn_swiglu",
    )(x, g, w_in, w_in, w_out)


def _moe_dense_kernel(x_ref, h_ref, gates_ref, wg_ref, wu_ref, wo_ref, out_ref, *, tc):
    e = pl.program_id(1)

    @pl.when((e == 0) & (pl.program_id(2) == 0))
    def _init():
        out_ref[...] = x_ref[...]
    gates = gates_ref[...]
    lane = lax.broadcasted_iota(jnp.int32, gates.shape, 1)
    ge = jnp.sum(jnp.where(lane == e, gates, 0.0), axis=-1, keepdims=True)
    _swiglu_acc(h_ref[...], wg_ref, wu_ref, wo_ref, tc, ge, out_ref)


def _moe_dense(x, h, gates, w_in, w_out, tm, tf, tc):
    T, D = x.shape
    E, F, _ = w_out.shape
    nf = F // tf
    return pl.pallas_call(
        functools.partial(_moe_dense_kernel, tc=tc),
        grid=(T // tm, E, nf),
        in_specs=[
            pl.BlockSpec((tm, D), lambda i, e, j: (i, 0)),
            pl.BlockSpec((tm, D), lambda i, e, j: (i, 0)),
            pl.BlockSpec((tm, LANE), lambda i, e, j: (i, 0)),
            pl.BlockSpec((1, D, tf), lambda i, e, j: (e, 0, j)),
            pl.BlockSpec((1, D, tf), lambda i, e, j: (e, 0, j + nf)),
            pl.BlockSpec((1, tf, D), lambda i, e, j: (e, j, 0)),
        ],
        out_specs=pl.BlockSpec((tm, D), lambda i, e, j: (i, 0)),
        out_shape=jax.ShapeDtypeStruct((T, D), F32),
        compiler_params=_params("parallel", "arbitrary", "arbitrary"),
        name="moe_experts",
    )(x, h, gates, w_in, w_in, w_out)


def _router_kernel(x_ref, g_ref, wr_ref, h_ref, gates_ref):
    h = _rms(x_ref[...], g_ref[...])
    h_hi = h.astype(BF16)
    h_ref[...] = h_hi
    h_lo = (h - h_hi.astype(F32)).astype(BF16)
    w = wr_ref[...]
    w_hi = w.astype(BF16)
    w_lo = (w - w_hi.astype(F32)).astype(BF16)
    logits = _dot(h_hi, w_hi) + (_dot(h_lo, w_hi) + _dot(h_hi, w_lo))
    lane = lax.broadcasted_iota(jnp.int32, logits.shape, 1)
    lg = jnp.where(lane < N_EXPERTS, logits, -jnp.inf)
    m1 = jnp.max(lg, axis=-1, keepdims=True)
    i1 = jnp.min(jnp.where(lg == m1, lane, LANE), axis=-1, keepdims=True)
    lg2 = jnp.where(lane == i1, -jnp.inf, lg)
    m2 = jnp.max(lg2, axis=-1, keepdims=True)
    i2 = jnp.min(jnp.where(lg2 == m2, lane, LANE), axis=-1, keepdims=True)
    e2 = jnp.exp(m2 - m1)
    w1 = 1.0 / (1.0 + e2)
    w2 = e2 / (1.0 + e2)
    gates_ref[...] = jnp.where(lane == i1, w1, 0.0) + jnp.where(lane == i2, w2, 0.0)


def _router(x, g, wr, tm):
    T, D = x.shape
    return pl.pallas_call(
        _router_kernel,
        grid=(T // tm,),
        in_specs=[
            pl.BlockSpec((tm, D), lambda i: (i, 0)),
            pl.BlockSpec((1, D), lambda i: (0, 0)),
            pl.BlockSpec((D, LANE), lambda i: (0, 0)),
        ],
        out_specs=[
            pl.BlockSpec((tm, D), lambda i: (i, 0)),
            pl.BlockSpec((tm, LANE), lambda i: (i, 0)),
        ],
        out_shape=[jax.ShapeDtypeStruct((T, D), BF16), jax.ShapeDtypeStruct((T, LANE), F32)],
        compiler_params=_params("parallel"),
        name="moe_router",
    )(x, g, wr)


def _glu_kernel(x_ref, g_ref, w_ref, b_ref, o_ref):
    h = _rms(x_ref[...], g_ref[...]).astype(BF16)
    u = _dot(h, w_ref[...]) + b_ref[...]
    d = o_ref.shape[-1]
    o_ref[...] = u[:, :d] * jax.nn.sigmoid(u[:, d:])


def _glu(x, g, w, b, tm):
    T, D = x.shape
    N = w.shape[1]
    return pl.pallas_call(
        _glu_kernel,
        grid=(T // tm,),
        in_specs=[
            pl.BlockSpec((tm, D), lambda i: (i, 0)),
            pl.BlockSpec((1, D), lambda i: (0, 0)),
            pl.BlockSpec((D, N), lambda i: (0, 0)),
            pl.BlockSpec((1, N), lambda i: (0, 0)),
        ],
        out_specs=pl.BlockSpec((tm, N // 2), lambda i: (i, 0)),
        out_shape=jax.ShapeDtypeStruct((T, N // 2), F32),
        compiler_params=_params("parallel"),
        name="conv_glu",
    )(x, g, w, b)


def _conv_kernel(u_ref, up_ref, un_ref, x_ref, wdw_ref, bdw_ref, g_ref, w2_ref, o_ref, buf, c_scr, *, ts, rc, cw):
    i = pl.program_id(1)
    n = pl.num_programs(1)
    buf[0:HALO, :] = jnp.where(i > 0, up_ref[0], 0.0)
    buf[HALO:HALO + ts, :] = u_ref[0]
    buf[HALO + ts:, :] = jnp.where(i < n - 1, un_ref[0], 0.0)
    off = HALO - CONV_PAD

    def body(c, carry):
        r0 = pl.multiple_of(c * rc, rc)
        for lc in range(buf.shape[1] // cw):
            cs = slice(lc * cw, (lc + 1) * cw)
            win = buf[pl.ds(r0, rc + 2 * HALO), cs]
            acc = jnp.zeros((rc, cw), F32)
            for k in range(CONV_WIDTH):
                acc = acc + win[off + k:off + k + rc, :] * wdw_ref[k:k + 1, cs]
            c_scr[pl.ds(r0, rc), cs] = acc
        return carry
    lax.fori_loop(0, ts // rc, body, 0)
    y = _rms(c_scr[...] + bdw_ref[...], g_ref[...])
    a = (y * jax.nn.sigmoid(y)).astype(BF16)
    o_ref[0] = x_ref[0] + _dot(a, w2_ref[...])


def _conv(u, x, wdw, bdw, g, w2, ts, rc):
    B, S, C = u.shape
    D = w2.shape[1]
    nh = ts // HALO
    last = S // HALO - 1
    return pl.pallas_call(
        functools.partial(_conv_kernel, ts=ts, rc=rc, cw=_pick(C, (256, 128))),
        grid=(B, S // ts),
        in_specs=[
            pl.BlockSpec((1, ts, C), lambda b, i: (b, i, 0)),
            pl.BlockSpec((1, HALO, C), lambda b, i: (b, jnp.maximum(i * nh - 1, 0), 0)),
            pl.BlockSpec((1, HALO, C), lambda b, i: (b, jnp.minimum((i + 1) * nh, last), 0)),
            pl.BlockSpec((1, ts, D), lambda b, i: (b, i, 0)),
            pl.BlockSpec((CONV_WIDTH, C), lambda b, i: (0, 0)),
            pl.BlockSpec((1, C), lambda b, i: (0, 0)),
            pl.BlockSpec((1, C), lambda b, i: (0, 0)),
            pl.BlockSpec((C, D), lambda b, i: (0, 0)),
        ],
        out_specs=pl.BlockSpec((1, ts, D), lambda b, i: (b, i, 0)),
        out_shape=jax.ShapeDtypeStruct((B, S, D), F32),
        scratch_shapes=[pltpu.VMEM((ts + 2 * HALO, C), F32), pltpu.VMEM((ts, C), F32)],
        compiler_params=_params("parallel", "parallel"),
        name="conv_module",
    )(u, u, u, x, wdw, bdw, g, w2)


def _rot_half(a):
    a1, a2 = jnp.split(a, 2, axis=-1)
    return jnp.concatenate([-a2, a1], axis=-1)


def _swap_half(a):
    a1, a2 = jnp.split(a, 2, axis=-1)
    return jnp.concatenate([a2, a1], axis=-1)


def _pad_lane(a):
    return jnp.pad(a, [(0, 0)] * (a.ndim - 1) + [(0, LANE - a.shape[-1])])


def _row(v):
    return v.reshape(1, -1).astype(F32)


def _prep_mla(w_in, q_lat_g, kv_lat_g, w_uq, w_ukv, q_norm_g, k_norm_g, w_o):
    o = Q_LORA + KV_LORA
    w_pe = w_in[:, o:]
    w_in_x = jnp.concatenate([w_in[:, :o], _pad_lane(w_pe), _pad_lane(_rot_half(w_pe))], axis=1).astype(BF16)
    wq = w_uq.reshape(Q_LORA, N_HEADS, QK_NOPE + QK_ROPE).transpose(1, 0, 2)
    wq_pe = wq[..., QK_NOPE:]
    wq_x = jnp.concatenate([wq[..., :QK_NOPE], _pad_lane(wq_pe), _pad_lane(_rot_half(wq_pe))], axis=-1).astype(BF16)
    wkv = w_ukv.reshape(KV_LORA, N_HEADS, QK_NOPE + V_HEAD).transpose(1, 0, 2).astype(BF16)
    return dict(
        w_in=w_in_x, gq=_row(q_lat_g), gkv=_row(kv_lat_g),
        gk=_row(_pad_lane(k_norm_g[QK_NOPE:])), gkp=_row(_pad_lane(_swap_half(k_norm_g[QK_NOPE:]))),
        wq=wq_x, wkv=wkv,
        gqn=_row(q_norm_g[:QK_NOPE]), gqp=_row(_pad_lane(q_norm_g[QK_NOPE:])),
        gqpp=_row(_pad_lane(_swap_half(q_norm_g[QK_NOPE:]))), gkn=_row(k_norm_g[:QK_NOPE]),
        w_o=w_o.astype(BF16))


def _rope_tables(seq):
    inv_freq = ROPE_THETA ** (-jnp.arange(0, QK_ROPE, 2, dtype=F32) / QK_ROPE)
    ang = jnp.arange(seq, dtype=F32)[:, None] * inv_freq[None, :]
    ang = jnp.concatenate([ang, ang], axis=-1)
    return _pad_lane(jnp.cos(ang)), _pad_lane(jnp.sin(ang))


def _pick(n, prefs):
    for p in prefs:
        if n % p == 0:
            return p
    return n


def _mla_layer(x, g_mix, p):
    B, S, D = x.shape
    cos, sin = _rope_tables(S)
    ts = _pick(S, (512, 256, 128))
    cq, ckv, kpe = _mla_in(x, g_mix, p["w_in"], p["gq"], p["gkv"], p["gk"], p["gkp"], cos, sin, ts)
    tq = _pick(S, (2048, 1024, 512, 256, 128))
    qc = _pick(tq, (256, 128))
    kc = _pick(S, (2048, 1024, 512, 256, 128))
    o = _attention(cq, ckv, kpe, p["wq"], p["wkv"], p["gqn"], p["gqp"], p["gqpp"], p["gkn"], cos, sin, tq, qc, kc)
    T = B * S
    y = _proj_res(o.reshape(T, -1), p["w_o"], x.reshape(T, D), _pick(T, (512, 256, 128)))
    return y.reshape(B, S, D)


def _ffn_layer(x, g, w_in, w_out):
    B, S, D = x.shape
    T = B * S
    F = w_out.shape[0]
    tf = _pick(F, (1408, 1024, 512, 256, 128))
    tc = _pick(tf, (512, 256, 128))
    y = _ffn(x.reshape(T, D), g, w_in[None], w_out[None], _pick(T, (1024, 512, 256, 128)), tf, tc)
    return y.reshape(B, S, D)


def _conv_layer(x, g_mix, w_pw1, b_pw1, w_dw, b_dw, g_conv, w_pw2):
    B, S, D = x.shape
    T = B * S
    u = _glu(x.reshape(T, D), g_mix, w_pw1, b_pw1, _pick(T, (512, 256, 128)))
    ts = _pick(S, (512, 256, 128))
    return _conv(u.reshape(B, S, -1), x, w_dw, b_dw, g_conv, w_pw2, ts, _pick(ts, (32, 16)))


def _moe_layer(x, g, w_router, w_in, w_out):
    B, S, D = x.shape
    T = B * S
    xt = x.reshape(T, D)
    h, gates = _router(xt, g, w_router, _pick(T, (512, 256, 128)))
    F = w_out.shape[1]
    tf = _pick(F, (896, 512, 256, 128))
    tc = _pick(tf, (512, 256, 128))
    y = _moe_dense(xt, h, gates, w_in, w_out, _pick(T, (1024, 512, 256, 128)), tf, tc)
    return y.reshape(B, S, D)


def kernel(x_prompt, x_sample, norm_mix_g, norm_ffn_g, mla_w_in, mla_q_lat_g, mla_kv_lat_g, mla_w_uq, mla_w_ukv, mla_q_norm_g, mla_k_norm_g, mla_w_o, conv_w_pw1, conv_b_pw1, conv_w_dw, conv_b_dw, conv_norm_g, conv_w_pw2, ffn_w_in, ffn_w_out, moe_w_router, moe_w_in, moe_w_out):
    depth = norm_mix_g.shape[0]
    mla = [_prep_mla(mla_w_in[j], mla_q_lat_g[j], mla_kv_lat_g[j], mla_w_uq[j], mla_w_ukv[j],
                     mla_q_norm_g[j], mla_k_norm_g[j], mla_w_o[j]) for j in range(mla_w_in.shape[0])]
    ffn_in, ffn_out = ffn_w_in.astype(BF16), ffn_w_out.astype(BF16)
    pw1, pw2 = conv_w_pw1.astype(BF16), conv_w_pw2.astype(BF16)
    moe_in, moe_out = moe_w_in.astype(BF16), moe_w_out.astype(BF16)
    router = _pad_lane(moe_w_router.astype(F32))

    def trunk(x):
        for i in range(depth):
            j = i // 2
            if i % 2 == 0:
                x = _mla_layer(x, _row(norm_mix_g[i]), mla[j])
                x = _ffn_layer(x, _row(norm_ffn_g[i]), ffn_in[j], ffn_out[j])
            else:
                x = _conv_layer(x, _row(norm_mix_g[i]), pw1[j], _row(conv_b_pw1[j]), conv_w_dw[j],
                                _row(conv_b_dw[j]), _row(conv_norm_g[j]), pw2[j])
                x = _moe_layer(x, _row(norm_ffn_g[i]), router[j], moe_in[j], moe_out[j])
        return x

    return (trunk(x_prompt), trunk(x_sample))
```

```python
import functools
import math

import jax
import jax.numpy as jnp
from jax import lax
from jax.experimental import pallas as pl
from jax.experimental.pallas import tpu as pltpu

N_HEADS = 16
QK_NOPE = 128
QK_ROPE = 64
V_HEAD = 128
Q_LORA = 256
KV_LORA = 128
ROPE_THETA = 10000.0
ATTN_SCALE = 1.0 / math.sqrt(QK_NOPE + QK_ROPE)
LOG2_E = math.log2(math.e)
CONV_WIDTH = 31
CONV_PAD = (CONV_WIDTH - 1) // 2
N_EXPERTS = 8
TOP_K = 2
RMS_EPS = 1e-6

LANE = 128
MXU = 256
HALO = 16
VMEM_LIMIT = 56 * 1024 * 1024
EXPERT_TILE_PREFS = (512, 256, 128)
ATTN_Q_PREFS = (2048, 1024, 512, 256, 128)
ATTN_KV_PREFS = (512, 256, 128)

F32 = jnp.float32
BF16 = jnp.bfloat16


def _params(*sem):
    return pltpu.CompilerParams(dimension_semantics=sem, vmem_limit_bytes=VMEM_LIMIT)


def _rms(xf, g):
    return xf * lax.rsqrt(jnp.mean(xf * xf, axis=-1, keepdims=True) + RMS_EPS) * g


def _rope_norm(p, r, g, gp, cos, sin):
    rs = lax.rsqrt(jnp.sum(p * p, axis=-1, keepdims=True) * (1.0 / QK_ROPE) + RMS_EPS)
    return (p * g * cos + r * gp * sin) * rs


def _dot(a, b):
    return jnp.dot(a, b, preferred_element_type=F32)


def _mla_in_kernel(x_ref, g_ref, w_ref, gq_ref, gkv_ref, gk_ref, gkp_ref, cos_ref, sin_ref,
                   cq_ref, ckv_ref, kpe_ref):
    h = _rms(x_ref[0], g_ref[...]).astype(BF16)
    a = _dot(h, w_ref[...])
    cq_ref[0] = _rms(a[:, :Q_LORA], gq_ref[...]).astype(BF16)
    ckv_ref[0] = _rms(a[:, Q_LORA:Q_LORA + KV_LORA], gkv_ref[...]).astype(BF16)
    o = Q_LORA + KV_LORA
    kpe = _rope_norm(a[:, o:o + LANE], a[:, o + LANE:o + 2 * LANE],
                     gk_ref[...], gkp_ref[...], cos_ref[...], sin_ref[...])
    kpe_ref[0] = kpe.astype(BF16)


def _mla_in(x, g, w, gq, gkv, gk, gkp, cos, sin, ts):
    B, S, D = x.shape
    n = w.shape[1]
    vec = lambda c: pl.BlockSpec((1, c), lambda b, i: (0, 0))
    return pl.pallas_call(
        _mla_in_kernel,
        grid=(B, S // ts),
        in_specs=[
            pl.BlockSpec((1, ts, D), lambda b, i: (b, i, 0)),
            vec(D),
            pl.BlockSpec((D, n), lambda b, i: (0, 0)),
            vec(Q_LORA), vec(KV_LORA), vec(LANE), vec(LANE),
            pl.BlockSpec((ts, LANE), lambda b, i: (i, 0)),
            pl.BlockSpec((ts, LANE), lambda b, i: (i, 0)),
        ],
        out_specs=[
            pl.BlockSpec((1, ts, Q_LORA), lambda b, i: (b, i, 0)),
            pl.BlockSpec((1, ts, KV_LORA), lambda b, i: (b, i, 0)),
            pl.BlockSpec((1, ts, LANE), lambda b, i: (b, i, 0)),
        ],
        out_shape=[
            jax.ShapeDtypeStruct((B, S, Q_LORA), BF16),
            jax.ShapeDtypeStruct((B, S, KV_LORA), BF16),
            jax.ShapeDtypeStruct((B, S, LANE), BF16),
        ],
        compiler_params=_params("parallel", "parallel"),
        name="mla_in",
    )(x, g, w, gq, gkv, gk, gkp, cos, sin)


def _attn_kernel(cq_ref, ckv_ref, kpe_ref, wq_ref, wkv_ref, gqn_ref, gqp_ref, gqpp_ref, gkn_ref,
                 cos_ref, sin_ref, o_ref, k_scr, v_scr, *, seq, kc, bc):
    @pl.when(pl.program_id(2) == 0)
    def _build_kv():
        lane = lax.broadcasted_iota(jnp.int32, (bc, LANE), 1)
        ones_col = jnp.where(lane == 0, 1.0, 0.0).astype(BF16)

        def body(i, carry):
            r0 = pl.multiple_of(i * bc, bc)
            kv = _dot(ckv_ref[0, pl.ds(r0, bc), :], wkv_ref[0])
            k_scr[pl.ds(r0, bc), 0:QK_NOPE] = _rms(kv[:, :QK_NOPE], gkn_ref[...]).astype(BF16)
            k_scr[pl.ds(r0, bc), QK_NOPE:QK_NOPE + LANE] = kpe_ref[0, pl.ds(r0, bc), :]
            v_scr[pl.ds(r0, bc), 0:V_HEAD] = kv[:, QK_NOPE:].astype(BF16)
            v_scr[pl.ds(r0, bc), V_HEAD:V_HEAD + LANE] = ones_col
            return carry
        lax.fori_loop(0, seq // bc, body, 0)

    q3 = _dot(cq_ref[0], wq_ref[0])
    qn = _rms(q3[:, :QK_NOPE], gqn_ref[...])
    qp = _rope_norm(q3[:, QK_NOPE:QK_NOPE + LANE], q3[:, QK_NOPE + LANE:],
                    gqp_ref[...], gqpp_ref[...], cos_ref[...], sin_ref[...])
    q = (jnp.concatenate([qn, qp], axis=-1) * (ATTN_SCALE * LOG2_E)).astype(BF16)
    tq = q.shape[0]

    def kv_body(c, carry):
        m, acc = carry
        r0 = pl.multiple_of(c * kc, kc)
        s = lax.dot_general(q, k_scr[pl.ds(r0, kc), :], (((1,), (1,)), ((), ())), preferred_element_type=F32)
        m_new = jnp.maximum(m, jnp.max(s, axis=-1, keepdims=True))
        p = jnp.exp2((s - m_new).astype(BF16))
        acc = jnp.exp2(m - m_new) * acc + _dot(p, v_scr[pl.ds(r0, kc), :])
        return m_new, acc
    n_kc = seq // kc
    _, acc = lax.fori_loop(0, n_kc, kv_body,
                           (jnp.full((tq, 1), -jnp.inf, F32), jnp.zeros((tq, V_HEAD + LANE), F32)),
                           unroll=min(n_kc, 4))
    o_ref[0] = (acc[:, :V_HEAD] / acc[:, V_HEAD:V_HEAD + 1]).astype(BF16)


def _attention(cq, ckv, kpe, wq, wkv, gqn, gqp, gqpp, gkn, cos, sin, tq, kc, bc):
    B, S, _ = cq.shape
    H = wq.shape[0]
    vec = lambda: pl.BlockSpec((1, LANE), lambda b, h, i: (0, 0))
    kern = functools.partial(_attn_kernel, seq=S, kc=kc, bc=bc)
    return pl.pallas_call(
        kern,
        grid=(B, H, S // tq),
        in_specs=[
            pl.BlockSpec((1, tq, Q_LORA), lambda b, h, i: (b, i, 0)),
            pl.BlockSpec((1, S, KV_LORA), lambda b, h, i: (b, 0, 0)),
            pl.BlockSpec((1, S, LANE), lambda b, h, i: (b, 0, 0)),
            pl.BlockSpec((1, Q_LORA, 3 * LANE), lambda b, h, i: (h, 0, 0)),
            pl.BlockSpec((1, KV_LORA, QK_NOPE + V_HEAD), lambda b, h, i: (h, 0, 0)),
            vec(), vec(), vec(), vec(),
            pl.BlockSpec((tq, LANE), lambda b, h, i: (i, 0)),
            pl.BlockSpec((tq, LANE), lambda b, h, i: (i, 0)),
        ],
        out_specs=pl.BlockSpec((1, tq, V_HEAD), lambda b, h, i: (b, i, h)),
        out_shape=jax.ShapeDtypeStruct((B, S, H * V_HEAD), BF16),
        scratch_shapes=[pltpu.VMEM((S, QK_NOPE + LANE), BF16), pltpu.VMEM((S, V_HEAD + LANE), BF16)],
        compiler_params=_params("parallel", "parallel", "arbitrary"),
        name="mla_attention",
    )(cq, ckv, kpe, wq, wkv, gqn, gqp, gqpp, gkn, cos, sin)


def _proj_res_kernel(a_ref, w_ref, x_ref, o_ref):
    o_ref[...] = x_ref[...] + _dot(a_ref[...], w_ref[...])


def _proj_res(a, w, x, tm):
    T, K = a.shape
    D = w.shape[1]
    return pl.pallas_call(
        _proj_res_kernel,
        grid=(T // tm,),
        in_specs=[
            pl.BlockSpec((tm, K), lambda i: (i, 0)),
            pl.BlockSpec((K, D), lambda i: (0, 0)),
            pl.BlockSpec((tm, D), lambda i: (i, 0)),
        ],
        out_specs=pl.BlockSpec((tm, D), lambda i: (i, 0)),
        out_shape=jax.ShapeDtypeStruct((T, D), F32),
        compiler_params=_params("parallel"),
        name="proj_residual",
    )(a, w, x)


def _swiglu_hidden(h, wg_ref, wu_ref, a_scr, tc):
    for c in range(a_scr.shape[1] // tc):
        sl = slice(c * tc, (c + 1) * tc)
        g = _dot(h, wg_ref[0, :, sl])
        u = _dot(h, wu_ref[0, :, sl])
        a_scr[:, sl] = (g * jax.nn.sigmoid(g) * u).astype(BF16)


def _ffn_kernel(x_ref, g_ref, wg_ref, wu_ref, wo_ref, out_ref, a_scr, *, tc):
    xf = x_ref[...]
    _swiglu_hidden(_rms(xf, g_ref[...]).astype(BF16), wg_ref, wu_ref, a_scr, tc)
    out_ref[...] = xf + _dot(a_scr[...], wo_ref[...])


def _ffn(x, g, w_in, w_out, tm, tc):
    T, D = x.shape
    F = w_out.shape[0]
    return pl.pallas_call(
        functools.partial(_ffn_kernel, tc=tc),
        grid=(T // tm,),
        in_specs=[
            pl.BlockSpec((tm, D), lambda i: (i, 0)),
            pl.BlockSpec((1, D), lambda i: (0, 0)),
            pl.BlockSpec((1, D, F), lambda i: (0, 0, 0)),
            pl.BlockSpec((1, D, F), lambda i: (0, 0, 1)),
            pl.BlockSpec((F, D), lambda i: (0, 0)),
        ],
        out_specs=pl.BlockSpec((tm, D), lambda i: (i, 0)),
        out_shape=jax.ShapeDtypeStruct((T, D), F32),
        scratch_shapes=[pltpu.VMEM((tm, F), BF16)],
        compiler_params=_params("parallel"),
        name="ffn_swiglu",
    )(x, g, w_in, w_in, w_out)


def _router_kernel(x_ref, g_ref, wr_ref, h_ref, meta_ref, gates_ref, counts_ref, carry):
    @pl.when(pl.program_id(0) == 0)
    def _init():
        carry[...] = jnp.zeros_like(carry)
    h = _rms(x_ref[...], g_ref[...])
    h_ref[...] = h
    h_hi = h.astype(BF16)
    h_lo = (h - h_hi.astype(F32)).astype(BF16)
    w = wr_ref[...]
    w_hi = w.astype(BF16)
    w_lo = (w - w_hi.astype(F32)).astype(BF16)
    logits = _dot(h_hi, w_hi) + (_dot(h_lo, w_hi) + _dot(h_hi, w_lo))
    tm = logits.shape[0]
    lane = lax.broadcasted_iota(jnp.int32, logits.shape, 1)
    lg = jnp.where(lane < N_EXPERTS, logits, -jnp.inf)
    m1 = jnp.max(lg, axis=-1, keepdims=True)
    i1 = jnp.min(jnp.where(lg == m1, lane, LANE), axis=-1, keepdims=True)
    lg2 = jnp.where(lane == i1, -jnp.inf, lg)
    m2 = jnp.max(lg2, axis=-1, keepdims=True)
    i2 = jnp.min(jnp.where(lg2 == m2, lane, LANE), axis=-1, keepdims=True)
    e2 = jnp.exp(m2 - m1)
    w1 = 1.0 / (1.0 + e2)
    w2 = e2 / (1.0 + e2)
    gates_ref[...] = jnp.where(lane == 0, w1, jnp.where(lane == 1, w2, 0.0))
    onehot = jnp.where((lane == i1) | (lane == i2), 1.0, 0.0)
    row = lax.broadcasted_iota(jnp.int32, (tm, tm), 0)
    col = lax.broadcasted_iota(jnp.int32, (tm, tm), 1)
    before = _dot(jnp.where(col < row, 1.0, 0.0).astype(BF16), onehot.astype(BF16)) + carry[...]
    r1 = jnp.sum(jnp.where(lane == i1, before, 0.0), axis=-1, keepdims=True).astype(jnp.int32)
    r2 = jnp.sum(jnp.where(lane == i2, before, 0.0), axis=-1, keepdims=True).astype(jnp.int32)
    meta_ref[...] = jnp.where(lane == 0, i1, jnp.where(lane == 1, i2, jnp.where(lane == 2, r1, jnp.where(lane == 3, r2, 0))))
    total = carry[...] + jnp.sum(onehot, axis=0, keepdims=True)
    carry[...] = total
    counts_ref[...] = total


def _router(x, g, wr, tm):
    T, D = x.shape
    return pl.pallas_call(
        _router_kernel,
        grid=(T // tm,),
        in_specs=[
            pl.BlockSpec((tm, D), lambda i: (i, 0)),
            pl.BlockSpec((1, D), lambda i: (0, 0)),
            pl.BlockSpec((D, LANE), lambda i: (0, 0)),
        ],
        out_specs=[
            pl.BlockSpec((tm, D), lambda i: (i, 0)),
            pl.BlockSpec((tm, LANE), lambda i: (i, 0)),
            pl.BlockSpec((tm, LANE), lambda i: (i, 0)),
            pl.BlockSpec((1, LANE), lambda i: (0, 0)),
        ],
        out_shape=[jax.ShapeDtypeStruct((T, D), F32), jax.ShapeDtypeStruct((T, LANE), jnp.int32),
                   jax.ShapeDtypeStruct((T, LANE), F32), jax.ShapeDtypeStruct((1, LANE), F32)],
        scratch_shapes=[pltpu.VMEM((1, LANE), F32)],
        compiler_params=_params("arbitrary"),
        name="moe_router",
    )(x, g, wr)


def _row_copy(src, dst, i, j, sem):
    return pltpu.make_async_copy(src.at[pl.ds(i, 1)], dst.at[pl.ds(j, 1)], sem)


def _dispatch_kernel(ends_ref, padded_ref, pos_ref, h_ref, xs_ref, zeros, sem, *, tm):
    td = h_ref.shape[0]

    @pl.when(pl.program_id(0) == 0)
    def _zero_padding():
        zeros[...] = jnp.zeros_like(zeros)

        def clear(start):
            start = start if isinstance(start, int) else pl.multiple_of(start, tm)
            cp = pltpu.make_async_copy(zeros, xs_ref.at[pl.ds(start, tm)], sem)
            cp.start()
            cp.wait()
        for e in range(N_EXPERTS):
            pl.when(padded_ref[e] > 0)(functools.partial(clear, ends_ref[e] - tm))
            unused = xs_ref.shape[0] - (e + 1) * tm
            pl.when(unused >= ends_ref[N_EXPERTS - 1])(functools.partial(clear, unused))

    def issue(t, c):
        _row_copy(h_ref, xs_ref, t, pos_ref[0, 0, 2 * t], sem).start()
        _row_copy(h_ref, xs_ref, t, pos_ref[0, 0, 2 * t + 1], sem).start()
        return c
    lax.fori_loop(0, td, issue, 0)

    def drain(t, c):
        _row_copy(h_ref, xs_ref, 0, 0, sem).wait()
        _row_copy(h_ref, xs_ref, 0, 0, sem).wait()
        return c
    lax.fori_loop(0, td, drain, 0)


def _dispatch(ends, padded, pos, h, rows, td, tm):
    T, D = h.shape
    return pl.pallas_call(
        functools.partial(_dispatch_kernel, tm=tm),
        grid_spec=pltpu.PrefetchScalarGridSpec(
            num_scalar_prefetch=2,
            grid=(T // td,),
            in_specs=[
                pl.BlockSpec((1, 1, 2 * td), lambda i, e, p: (i, 0, 0), memory_space=pltpu.SMEM),
                pl.BlockSpec((td, D), lambda i, e, p: (i, 0)),
            ],
            out_specs=pl.BlockSpec(memory_space=pl.ANY),
            scratch_shapes=[pltpu.VMEM((tm, D), F32), pltpu.SemaphoreType.DMA],
        ),
        out_shape=jax.ShapeDtypeStruct((rows, D), F32),
        compiler_params=_params("arbitrary"),
        name="moe_dispatch",
    )(ends, padded, pos, h)


def _experts_kernel(te_ref, nv_ref, xs_ref, wg_ref, wu_ref, wo_ref, ys_ref, h_scr, a_scr, *, tc):
    j = pl.program_id(1)

    @pl.when(pl.program_id(0) >= nv_ref[0])
    def _unused_tile():
        ys_ref[...] = jnp.zeros_like(ys_ref)

    @pl.when(pl.program_id(0) < nv_ref[0])
    def _tile():
        @pl.when(j == 0)
        def _cast():
            h_scr[...] = xs_ref[...].astype(BF16)
        _swiglu_hidden(h_scr[...], wg_ref, wu_ref, a_scr, tc)
        d = _dot(a_scr[...], wo_ref[0])

        @pl.when(j == 0)
        def _first():
            ys_ref[...] = d

        @pl.when(j > 0)
        def _rest():
            ys_ref[...] += d


def _experts(te, nv, xs, w_in, w_out, tm, tf, tc):
    rows, D = xs.shape
    E, F, _ = w_out.shape
    nf = F // tf
    tile = lambda r, nv: jnp.minimum(r, nv[0] - 1)
    chunk = lambda r, j, nv: jnp.where(r < nv[0], j, nf - 1)
    return pl.pallas_call(
        functools.partial(_experts_kernel, tc=tc),
        grid_spec=pltpu.PrefetchScalarGridSpec(
            num_scalar_prefetch=2,
            grid=(rows // tm, nf),
            in_specs=[
                pl.BlockSpec((tm, D), lambda r, j, te, nv: (tile(r, nv), 0)),
                pl.BlockSpec((1, D, tf), lambda r, j, te, nv: (te[tile(r, nv)], 0, chunk(r, j, nv))),
                pl.BlockSpec((1, D, tf), lambda r, j, te, nv: (te[tile(r, nv)], 0, chunk(r, j, nv) + nf)),
                pl.BlockSpec((1, tf, D), lambda r, j, te, nv: (te[tile(r, nv)], chunk(r, j, nv), 0)),
            ],
            out_specs=pl.BlockSpec((tm, D), lambda r, j, te, nv: (r, 0)),
            scratch_shapes=[pltpu.VMEM((tm, D), BF16), pltpu.VMEM((tm, tf), BF16)],
        ),
        out_shape=jax.ShapeDtypeStruct((rows, D), F32),
        compiler_params=_params("arbitrary", "arbitrary"),
        name="moe_experts",
    )(te, nv, xs, w_in, w_in, w_out)


def _combine_kernel(pos_ref, x_ref, gates_ref, ys_ref, out_ref, ybuf, sem):
    tcb = x_ref.shape[0]

    def issue(t, c):
        _row_copy(ys_ref, ybuf.at[0], pos_ref[0, 0, 2 * t], t, sem).start()
        _row_copy(ys_ref, ybuf.at[1], pos_ref[0, 0, 2 * t + 1], t, sem).start()
        return c
    lax.fori_loop(0, tcb, issue, 0)

    def drain(t, c):
        _row_copy(ys_ref, ybuf.at[0], 0, 0, sem).wait()
        _row_copy(ys_ref, ybuf.at[1], 0, 0, sem).wait()
        return c
    lax.fori_loop(0, tcb, drain, 0)
    w = gates_ref[...]
    out_ref[...] = x_ref[...] + (w[:, 0:1] * ybuf[0] + w[:, 1:2] * ybuf[1])


def _combine(pos, x, gates, ys, tcb):
    T, D = x.shape
    return pl.pallas_call(
        _combine_kernel,
        grid=(T // tcb,),
        in_specs=[
            pl.BlockSpec((1, 1, 2 * tcb), lambda i: (i, 0, 0), memory_space=pltpu.SMEM),
            pl.BlockSpec((tcb, D), lambda i: (i, 0)),
            pl.BlockSpec((tcb, LANE), lambda i: (i, 0)),
            pl.BlockSpec(memory_space=pl.ANY),
        ],
        out_specs=pl.BlockSpec((tcb, D), lambda i: (i, 0)),
        out_shape=jax.ShapeDtypeStruct((T, D), F32),
        scratch_shapes=[pltpu.VMEM((2, tcb, D), F32), pltpu.SemaphoreType.DMA],
        compiler_params=_params("parallel"),
        name="moe_combine",
    )(pos, x, gates, ys)


def _glu_kernel(x_ref, g_ref, w_ref, b_ref, o_ref):
    h = _rms(x_ref[...], g_ref[...]).astype(BF16)
    u = _dot(h, w_ref[...]) + b_ref[...]
    d = o_ref.shape[-1]
    o_ref[...] = u[:, :d] * jax.nn.sigmoid(u[:, d:])


def _glu(x, g, w, b, tm):
    T, D = x.shape
    N = w.shape[1]
    return pl.pallas_call(
        _glu_kernel,
        grid=(T // tm,),
        in_specs=[
            pl.BlockSpec((tm, D), lambda i: (i, 0)),
            pl.BlockSpec((1, D), lambda i: (0, 0)),
            pl.BlockSpec((D, N), lambda i: (0, 0)),
            pl.BlockSpec((1, N), lambda i: (0, 0)),
        ],
        out_specs=pl.BlockSpec((tm, N // 2), lambda i: (i, 0)),
        out_shape=jax.ShapeDtypeStruct((T, N // 2), F32),
        compiler_params=_params("parallel"),
        name="conv_glu",
    )(x, g, w, b)


def _conv_kernel(u_ref, up_ref, un_ref, x_ref, wdw_ref, bdw_ref, g_ref, w2_ref, o_ref, buf, c_scr, *, ts, rc, cw):
    i = pl.program_id(1)
    n = pl.num_programs(1)
    buf[0:HALO, :] = jnp.where(i > 0, up_ref[0], 0.0)
    buf[HALO:HALO + ts, :] = u_ref[0]
    buf[HALO + ts:, :] = jnp.where(i < n - 1, un_ref[0], 0.0)
    off = HALO - CONV_PAD

    def body(c, carry):
        r0 = pl.multiple_of(c * rc, rc)
        for lc in range(buf.shape[1] // cw):
            cs = slice(lc * cw, (lc + 1) * cw)
            win = buf[pl.ds(r0, rc + 2 * HALO), cs]
            acc = jnp.zeros((rc, cw), F32)
            for k in range(CONV_WIDTH):
                acc = acc + win[off + k:off + k + rc, :] * wdw_ref[k:k + 1, cs]
            c_scr[pl.ds(r0, rc), cs] = acc
        return carry
    lax.fori_loop(0, ts // rc, body, 0)
    y = _rms(c_scr[...] + bdw_ref[...], g_ref[...])
    a = (y * jax.nn.sigmoid(y)).astype(BF16)
    o_ref[0] = x_ref[0] + _dot(a, w2_ref[...])


def _conv(u, x, wdw, bdw, g, w2, ts, rc):
    B, S, C = u.shape
    D = w2.shape[1]
    nh = ts // HALO
    last = S // HALO - 1
    return pl.pallas_call(
        functools.partial(_conv_kernel, ts=ts, rc=rc, cw=_pick(C, (256, 128))),
        grid=(B, S // ts),
        in_specs=[
            pl.BlockSpec((1, ts, C), lambda b, i: (b, i, 0)),
            pl.BlockSpec((1, HALO, C), lambda b, i: (b, jnp.maximum(i * nh - 1, 0), 0)),
            pl.BlockSpec((1, HALO, C), lambda b, i: (b, jnp.minimum((i + 1) * nh, last), 0)),
            pl.BlockSpec((1, ts, D), lambda b, i: (b, i, 0)),
            pl.BlockSpec((CONV_WIDTH, C), lambda b, i: (0, 0)),
            pl.BlockSpec((1, C), lambda b, i: (0, 0)),
            pl.BlockSpec((1, C), lambda b, i: (0, 0)),
            pl.BlockSpec((C, D), lambda b, i: (0, 0)),
        ],
        out_specs=pl.BlockSpec((1, ts, D), lambda b, i: (b, i, 0)),
        out_shape=jax.ShapeDtypeStruct((B, S, D), F32),
        scratch_shapes=[pltpu.VMEM((ts + 2 * HALO, C), F32), pltpu.VMEM((ts, C), F32)],
        compiler_params=_params("parallel", "parallel"),
        name="conv_module",
    )(u, u, u, x, wdw, bdw, g, w2)


def _rot_half(a):
    a1, a2 = jnp.split(a, 2, axis=-1)
    return jnp.concatenate([-a2, a1], axis=-1)


def _swap_half(a):
    a1, a2 = jnp.split(a, 2, axis=-1)
    return jnp.concatenate([a2, a1], axis=-1)


def _pad_lane(a):
    return jnp.pad(a, [(0, 0)] * (a.ndim - 1) + [(0, LANE - a.shape[-1])])


def _row(v):
    return v.reshape(1, -1).astype(F32)


def _prep_mla(w_in, q_lat_g, kv_lat_g, w_uq, w_ukv, q_norm_g, k_norm_g, w_o):
    o = Q_LORA + KV_LORA
    w_pe = w_in[:, o:]
    w_in_x = jnp.concatenate([w_in[:, :o], _pad_lane(w_pe), _pad_lane(_rot_half(w_pe))], axis=1).astype(BF16)
    wq = w_uq.reshape(Q_LORA, N_HEADS, QK_NOPE + QK_ROPE).transpose(1, 0, 2)
    wq_pe = wq[..., QK_NOPE:]
    wq_x = jnp.concatenate([wq[..., :QK_NOPE], _pad_lane(wq_pe), _pad_lane(_rot_half(wq_pe))], axis=-1).astype(BF16)
    wkv = w_ukv.reshape(KV_LORA, N_HEADS, QK_NOPE + V_HEAD).transpose(1, 0, 2).astype(BF16)
    return dict(
        w_in=w_in_x, gq=_row(q_lat_g), gkv=_row(kv_lat_g),
        gk=_row(_pad_lane(k_norm_g[QK_NOPE:])), gkp=_row(_pad_lane(_swap_half(k_norm_g[QK_NOPE:]))),
        wq=wq_x, wkv=wkv,
        gqn=_row(q_norm_g[:QK_NOPE]), gqp=_row(_pad_lane(q_norm_g[QK_NOPE:])),
        gqpp=_row(_pad_lane(_swap_half(q_norm_g[QK_NOPE:]))), gkn=_row(k_norm_g[:QK_NOPE]),
        w_o=w_o.astype(BF16))


def _rope_tables(seq):
    inv_freq = ROPE_THETA ** (-jnp.arange(0, QK_ROPE, 2, dtype=F32) / QK_ROPE)
    ang = jnp.arange(seq, dtype=F32)[:, None] * inv_freq[None, :]
    ang = jnp.concatenate([ang, ang], axis=-1)
    return _pad_lane(jnp.cos(ang)), _pad_lane(jnp.sin(ang))


def _pick(n, prefs):
    for p in prefs:
        if n % p == 0:
            return p
    return n


def _mla_layer(x, g_mix, p):
    B, S, D = x.shape
    cos, sin = _rope_tables(S)
    ts = _pick(S, (512, 256, 128))
    cq, ckv, kpe = _mla_in(x, g_mix, p["w_in"], p["gq"], p["gkv"], p["gk"], p["gkp"], cos, sin, ts)
    tq = _pick(S, ATTN_Q_PREFS)
    kc = _pick(S, ATTN_KV_PREFS)
    bc = _pick(S, (2048, 1024, 512, 256, 128))
    o = _attention(cq, ckv, kpe, p["wq"], p["wkv"], p["gqn"], p["gqp"], p["gqpp"], p["gkn"], cos, sin, tq, kc, bc)
    T = B * S
    y = _proj_res(o.reshape(T, -1), p["w_o"], x.reshape(T, D), _pick(T, (512, 256, 128)))
    return y.reshape(B, S, D)


def _ffn_layer(x, g, w_in, w_out):
    B, S, D = x.shape
    T = B * S
    y = _ffn(x.reshape(T, D), g, w_in[None], w_out, _pick(T, (512, 256, 128)), _pick(w_out.shape[0], (MXU, LANE)))
    return y.reshape(B, S, D)


def _conv_layer(x, g_mix, w_pw1, b_pw1, w_dw, b_dw, g_conv, w_pw2):
    B, S, D = x.shape
    T = B * S
    u = _glu(x.reshape(T, D), g_mix, w_pw1, b_pw1, _pick(T, (512, 256, 128)))
    ts = _pick(S, (512, 256, 128))
    return _conv(u.reshape(B, S, -1), x, w_dw, b_dw, g_conv, w_pw2, ts, _pick(ts, (32, 16)))


def _moe_layer(x, g, w_router, w_in, w_out):
    B, S, D = x.shape
    T = B * S
    E, F, _ = w_out.shape
    xt = x.reshape(T, D)
    tt = _pick(T, (512, 256, 128))
    tm = _pick(TOP_K * T, EXPERT_TILE_PREFS)
    h, meta, gates, counts = _router(xt, g, w_router, tt)
    cnt = counts[0, :E].astype(jnp.int32)
    padded = (cnt + tm - 1) // tm * tm
    ends = jnp.cumsum(padded)
    pos = jnp.take(ends - padded, meta[:, 0:TOP_K]) + meta[:, TOP_K:2 * TOP_K]
    n_tiles = TOP_K * T // tm + E
    te = jnp.minimum(jnp.searchsorted(ends, jnp.arange(n_tiles, dtype=jnp.int32) * tm, side="right"), E - 1)
    nv = (ends[-1:] // tm).astype(jnp.int32)
    pos = pos.astype(jnp.int32).reshape(T // tt, 1, TOP_K * tt)
    xs = _dispatch(ends.astype(jnp.int32), padded, pos, h, n_tiles * tm, tt, tm)
    tf = _pick(F, (1792, 1024, 512, 256, 128))
    ys = _experts(te.astype(jnp.int32), nv, xs, w_in, w_out, tm, tf, _pick(tf, (MXU, LANE)))
    return _combine(pos, xt, gates, ys, tt).reshape(B, S, D)


def kernel(x_prompt, x_sample, norm_mix_g, norm_ffn_g, mla_w_in, mla_q_lat_g, mla_kv_lat_g, mla_w_uq, mla_w_ukv, mla_q_norm_g, mla_k_norm_g, mla_w_o, conv_w_pw1, conv_b_pw1, conv_w_dw, conv_b_dw, conv_norm_g, conv_w_pw2, ffn_w_in, ffn_w_out, moe_w_router, moe_w_in, moe_w_out):
    depth = norm_mix_g.shape[0]
    mla = [_prep_mla(mla_w_in[j], mla_q_lat_g[j], mla_kv_lat_g[j], mla_w_uq[j], mla_w_ukv[j],
                     mla_q_norm_g[j], mla_k_norm_g[j], mla_w_o[j]) for j in range(mla_w_in.shape[0])]
    ffn_in, ffn_out = ffn_w_in.astype(BF16), ffn_w_out.astype(BF16)
    pw1, pw2 = conv_w_pw1.astype(BF16), conv_w_pw2.astype(BF16)
    moe_in, moe_out = moe_w_in.astype(BF16), moe_w_out.astype(BF16)
    router = _pad_lane(moe_w_router.astype(F32))

    def trunk(x):
        for i in range(depth):
            j = i // 2
            if i % 2 == 0:
                x = _mla_layer(x, _row(norm_mix_g[i]), mla[j])
                x = _ffn_layer(x, _row(norm_ffn_g[i]), ffn_in[j], ffn_out[j])
            else:
                x = _conv_layer(x, _row(norm_mix_g[i]), pw1[j], _row(conv_b_pw1[j]), conv_w_dw[j],
                                _row(conv_b_dw[j]), _row(conv_norm_g[j]), pw2[j])
                x = _moe_layer(x, _row(norm_ffn_g[i]), router[j], moe_in[j], moe_out[j])
        return x

    y_sample = trunk(x_sample)
    return (trunk(x_prompt), y_sample)
```
